```python
import math
import jax
import jax.numpy as jnp
from jax import lax
import numpy as np

D_MODEL = 1024
BATCH = 4
SEQ = 8192
DEPTH = 2
DEC_BATCH = 32
DEC_SEQ = 1
PAST_LEN = 16384
PAGE_SIZE = 128

N_MEM = 256
MEM_HEADS = 4
MEM_HD = D_MODEL // MEM_HEADS
FOX_HEADS = 8
FOX_HD = 64
FOX_WIDTH = FOX_HEADS * FOX_HD
Q_BLOCK = 128
POOL_WINDOWS = (2, 4, 8, 16)
POOL_GROUPS = len(POOL_WINDOWS)
POOL_WIDTH = D_MODEL // 2
POOL_GC = POOL_WIDTH // POOL_GROUPS
POOL_BUF = max(POOL_WINDOWS) - 1
SSM_HEADS = 16
SSM_HD = 64
SSM_INNER = SSM_HEADS * SSM_HD
SSM_GROUPS = 2
SSM_STATE = 128
CONV_W = 4
SSM_CHUNK = 128
CONV_CH = SSM_INNER + 2 * SSM_GROUPS * SSM_STATE
GMLP_WIDTH = D_MODEL // 2
GMLP_GROUPS = 4
GMLP_GC = GMLP_WIDTH // GMLP_GROUPS
GMLP_CHUNK = 128
D_FF = -(-8 * D_MODEL // (3 * 256)) * 256
ALPHA = (2 * DEPTH) ** 0.25
BETA = (8 * DEPTH) ** -0.25
LN_EPS = 1e-5
N_AB = (DEPTH + 1) // 2
N_CD = DEPTH // 2
AB_IN = 3 * FOX_WIDTH + FOX_HEADS + POOL_WIDTH
AB_OUT = FOX_WIDTH + POOL_WIDTH
CD_IN = SSM_INNER + CONV_CH + SSM_HEADS + 2 * GMLP_WIDTH
CD_OUT = SSM_INNER + GMLP_WIDTH

kernel_name = 'hybrid_fox_pool_ssd_gmlp_step'


def _ln(x, g, b):
    xf = x.astype(jnp.float32)
    mu = jnp.mean(xf, axis=-1, keepdims=True)
    var = jnp.mean(jnp.square(xf - mu), axis=-1, keepdims=True)
    return ((xf - mu) * lax.rsqrt(var + LN_EPS) * g.astype(jnp.float32) + b.astype(jnp.float32)).astype(x.dtype)


def _rmsnorm(x, w):
    xf = x.astype(jnp.float32)
    return xf * lax.rsqrt(jnp.mean(jnp.square(xf), axis=-1, keepdims=True) + LN_EPS) * w.astype(jnp.float32)


def _post(x, sub, g, b):
    return _ln(ALPHA * x + sub, g, b)


def _swiglu(x, wg, wu, wd):
    return (jax.nn.silu(x @ wg) * (x @ wu)) @ wd


def _mem_kv(mem, wk, wv):
    b, m, _ = mem.shape
    return (mem @ wk).reshape(b, m, MEM_HEADS, MEM_HD), (mem @ wv).reshape(b, m, MEM_HEADS, MEM_HD)


def _cross_attn(x, mk, mv, wq, wo):
    b, l, _ = x.shape
    q = (x @ wq).reshape(b, l, MEM_HEADS, MEM_HD)
    s = jnp.einsum('bqhd,bkhd->bhqk', q, mk).astype(jnp.float32) * (MEM_HD ** -0.5)
    p = jax.nn.softmax(s, axis=-1).astype(mv.dtype)
    o = jnp.einsum('bhqk,bkhd->bqhd', p, mv).reshape(b, l, MEM_HEADS * MEM_HD)
    return o @ wo


def _ab_project(x, w_in, b_forget):
    b, l, _ = x.shape
    proj = x @ w_in
    q, k, v, fl, u = jnp.split(proj, [FOX_WIDTH, 2 * FOX_WIDTH, 3 * FOX_WIDTH, 3 * FOX_WIDTH + FOX_HEADS], axis=-1)
    logf = jax.nn.log_sigmoid(fl.astype(jnp.float32) + b_forget.astype(jnp.float32))
    shp = (b, l, FOX_HEADS, FOX_HD)
    return q.reshape(shp), k.reshape(shp), v.reshape(shp), logf, u


def _fox_prompt(q, k, v, logf):
    b, s, h, d = q.shape
    nb = s // Q_BLOCK
    c = jnp.cumsum(logf.astype(jnp.float32), axis=1).transpose(0, 2, 1)
    qb = q.reshape(b, nb, Q_BLOCK, h, d).transpose(1, 0, 2, 3, 4)
    cb = c.reshape(b, h, nb, Q_BLOCK).transpose(2, 0, 1, 3)
    kpos = jnp.arange(s)
    scale = FOX_HD ** -0.5

    def block(args):
        i, qi, ci = args
        sc = jnp.einsum('bqhd,bkhd->bhqk', qi, k).astype(jnp.float32) * scale
        sc = sc + ci[..., :, None] - c[:, :, None, :]
        qpos = i * Q_BLOCK + jnp.arange(Q_BLOCK)
        sc = jnp.where(kpos[None, :] <= qpos[:, None], sc, -jnp.inf)
        p = jax.nn.softmax(sc, axis=-1).astype(v.dtype)
        return jnp.einsum('bhqk,bkhd->bqhd', p, v)

    out = lax.map(block, (jnp.arange(nb), qb, cb))
    return out.transpose(1, 0, 2, 3, 4).reshape(b, s, h * d)


def _fox_sample(q, k_new, v_new, logf_new, k_pool, v_pool, lf_pool, page_table):
    b, l, h, d = q.shape
    past = page_table.shape[1] * k_pool.shape[1]
    k_all = jnp.concatenate([k_pool[page_table].reshape(b, past, h, d).astype(k_new.dtype), k_new], axis=1)
    v_all = jnp.concatenate([v_pool[page_table].reshape(b, past, h, d).astype(v_new.dtype), v_new], axis=1)
    lf_all = jnp.concatenate([lf_pool[page_table].reshape(b, past, h).astype(jnp.float32),
                              logf_new.astype(jnp.float32)], axis=1)
    c = jnp.cumsum(lf_all, axis=1).transpose(0, 2, 1)
    sc = jnp.einsum('bqhd,bkhd->bhqk', q, k_all).astype(jnp.float32) * (FOX_HD ** -0.5)
    sc = sc + c[:, :, past:, None] - c[:, :, None, :]
    qpos = past + jnp.arange(l)
    kpos = jnp.arange(past + l)
    sc = jnp.where(kpos[None, :] <= qpos[:, None], sc, -jnp.inf)
    p = jax.nn.softmax(sc, axis=-1).astype(v_all.dtype)
    return jnp.einsum('bhqk,bkhd->bqhd', p, v_all).reshape(b, l, h * d)


def _pool_mixer(u, prefix, start, w_pool, pool_scale):
    b, l, ch = u.shape
    ext = jnp.concatenate([prefix.astype(u.dtype), u], axis=1)
    cs = jnp.cumsum(ext.astype(jnp.float32), axis=1)
    cs = jnp.concatenate([jnp.zeros((b, 1, ch), jnp.float32), cs], axis=1)
    hi = cs[:, POOL_BUF + 1:]
    pos = start + jnp.arange(l)
    uf = u.astype(jnp.float32)
    outs = []
    for g, w in enumerate(POOL_WINDOWS):
        sl = slice(g * POOL_GC, (g + 1) * POOL_GC)
        lo = cs[:, POOL_BUF + 1 - w:POOL_BUF + 1 - w + l, sl]
        cnt = jnp.minimum(pos + 1, w).astype(jnp.float32)[None, :, None]
        outs.append((hi[..., sl] - lo) / cnt - uf[..., sl])
    dlt = jnp.stack(outs, axis=2).astype(u.dtype)
    y = jnp.einsum('blgc,gce->blge', dlt, w_pool).reshape(b, l, ch) * pool_scale
    return y, ext[:, -POOL_BUF:]


def _causal_conv(xbc, prefix, w, bias):
    ext = jnp.concatenate([prefix.astype(xbc.dtype), xbc], axis=1)
    y = lax.conv_general_dilated(ext, w[:, None, :].astype(ext.dtype), (1,), 'VALID',
                                 dimension_numbers=('NWC', 'WIO', 'NWC'),
                                 feature_group_count=ext.shape[-1])
    return y + bias, ext[:, -(CONV_W - 1):]


def _ssd(xdt, da, bm, cm, h0):
    b, l, nh, hp = xdt.shape
    ng, ns = bm.shape[2], bm.shape[3]
    nr = nh // ng
    lc = min(SSM_CHUNK, l)
    nc = -(-l // lc)
    pad = nc * lc - l
    f32 = jnp.float32
    if pad:
        def padf(a):
            return jnp.pad(a, [(0, 0), (0, pad)] + [(0, 0)] * (a.ndim - 2))
        xdt, da, bm, cm = padf(xdt), padf(da), padf(bm), padf(cm)
    xc = xdt.astype(f32).reshape(b, nc, lc, ng, nr, hp)
    ac = da.astype(f32).reshape(b, nc, lc, ng, nr)
    bc = bm.astype(f32).reshape(b, nc, lc, ng, ns)
    cc = cm.astype(f32).reshape(b, nc, lc, ng, ns)
    acs = jnp.cumsum(ac, axis=2)
    causal = jnp.tril(jnp.ones((lc, lc), bool))[None, None, :, :, None, None]
    seg = jnp.exp(jnp.where(causal, acs[:, :, :, None] - acs[:, :, None, :], -jnp.inf))
    cb = jnp.einsum('bctgn,bcsgn->bctsg', cc, bc)
    y_diag = jnp.einsum('bctsg,bctsgr,bcsgrp->bctgrp', cb, seg, xc)
    decay_to_end = jnp.exp(acs[:, :, -1:] - acs)
    chunk_states = jnp.einsum('bclgn,bclgr,bclgrp->bcgrpn', bc, decay_to_end, xc)
    chunk_decay = jnp.exp(acs[:, :, -1])

    def step(h, inp):
        s_c, a_c = inp
        return h * a_c[..., None, None] + s_c, h

    h_last, h_prev = lax.scan(step, h0.astype(f32).reshape(b, ng, nr, hp, ns),
                              (jnp.moveaxis(chunk_states, 1, 0), jnp.moveaxis(chunk_decay, 1, 0)))
    h_prev = jnp.moveaxis(h_prev, 0, 1)
    y_off = jnp.einsum('bclgn,bcgrpn,bclgr->bclgrp', cc, h_prev, jnp.exp(acs))
    y = (y_diag + y_off).reshape(b, nc * lc, nh, hp)[:, :l]
    return y, h_last.reshape(b, nh, hp, ns)


def _cd_mixer(x, conv_prefix, h0, w_in, conv_w, conv_b, dt_bias, a_log, d_skip, norm_w,
              ln_g, ln_b, w_sp, b_sp, w_out):
    b, l, _ = x.shape
    proj = x @ w_in
    o1 = SSM_INNER
    o2 = o1 + CONV_CH
    o3 = o2 + SSM_HEADS
    o4 = o3 + GMLP_WIDTH
    z, xbc, dt_raw, gu, gv = jnp.split(proj, [o1, o2, o3, o4], axis=-1)
    xbc, conv_state = _causal_conv(xbc, conv_prefix, conv_w, conv_b)
    xbc = jax.nn.silu(xbc)
    xs, bm, cm = jnp.split(xbc, [SSM_INNER, SSM_INNER + SSM_GROUPS * SSM_STATE], axis=-1)
    xs = xs.reshape(b, l, SSM_HEADS, SSM_HD)
    bm = bm.reshape(b, l, SSM_GROUPS, SSM_STATE)
    cm = cm.reshape(b, l, SSM_GROUPS, SSM_STATE)
    dt = jax.nn.softplus(dt_raw.astype(jnp.float32) + dt_bias.astype(jnp.float32))
    da = dt * (-jnp.exp(a_log.astype(jnp.float32)))
    y, h_last = _ssd(xs.astype(jnp.float32) * dt[..., None], da, bm, cm, h0)
    y = y + d_skip.astype(jnp.float32)[:, None] * xs.astype(jnp.float32)
    y = _rmsnorm(y.reshape(b, l, SSM_INNER) * jax.nn.silu(z.astype(jnp.float32)), norm_w).astype(x.dtype)
    u = jax.nn.gelu(gu)
    v = _ln(jax.nn.gelu(gv), ln_g, ln_b)
    lc = min(GMLP_CHUNK, l)
    nch = l // lc
    wm = w_sp[:, :lc, :lc] * jnp.tril(jnp.ones((lc, lc), w_sp.dtype))
    vc = v.reshape(b, nch, lc, GMLP_GROUPS, GMLP_GC)
    sg = jnp.einsum('gts,bnsgc->bntgc', wm, vc) + b_sp[:, :lc].T[None, None, :, :, None]
    yd = (u * sg.reshape(b, l, GMLP_WIDTH)).astype(x.dtype)
    out = jnp.concatenate([y, yd], axis=-1) @ w_out
    return out, conv_state, h_last, v


def setup_inputs(seed: int = 0) -> dict:
    key = jax.random.key(seed)
    keys = jax.random.split(key, 64)
    counter = [0]

    def nk():
        counter[0] += 1
        return keys[counter[0] - 1]

    f32 = jnp.float32
    n_pages = PAST_LEN // PAGE_SIZE
    n_used = DEC_BATCH * n_pages
    n_phys = n_used + (n_used + 3) // 4

    def w(shape, fan_in, gain=1.0):
        return jax.random.normal(nk(), shape, f32) * (gain * fan_in ** -0.5)

    def gain(shape):
        return 1.0 + 0.02 * jax.random.normal(nk(), shape, f32)

    def small(shape, s=0.02):
        return s * jax.random.normal(nk(), shape, f32)

    inp = {}
    inp['x_prompt'] = jax.random.normal(nk(), (BATCH, SEQ, D_MODEL), f32)
    inp['x_sample'] = jax.random.normal(nk(), (DEC_BATCH, DEC_SEQ, D_MODEL), f32)
    inp['cache_fox_k'] = jax.random.normal(nk(), (N_AB, n_phys, PAGE_SIZE, FOX_HEADS, FOX_HD), f32)
    inp['cache_fox_v'] = jax.random.normal(nk(), (N_AB, n_phys, PAGE_SIZE, FOX_HEADS, FOX_HD), f32)
    inp['cache_fox_logf'] = jax.nn.log_sigmoid(
        jax.random.normal(nk(), (N_AB, n_phys, PAGE_SIZE, FOX_HEADS), f32) + 4.0)
    inp['state_pool'] = jax.random.normal(nk(), (N_AB, DEC_BATCH, POOL_BUF, POOL_WIDTH), f32)
    inp['state_conv'] = jax.random.normal(nk(), (N_CD, DEC_BATCH, CONV_W - 1, CONV_CH), f32)
    inp['state_ssm'] = 0.1 * jax.random.normal(nk(), (N_CD, DEC_BATCH, SSM_HEADS, SSM_HD, SSM_STATE), f32)
    inp['cache_mem_k'] = jax.random.normal(nk(), (DEPTH, DEC_BATCH, N_MEM, MEM_HEADS, MEM_HD), f32)
    inp['cache_mem_v'] = jax.random.normal(nk(), (DEPTH, DEC_BATCH, N_MEM, MEM_HEADS, MEM_HD), f32)
    inp['page_table'] = jax.random.permutation(nk(), n_phys)[:n_used].reshape(DEC_BATCH, n_pages).astype(jnp.int32)
    inp['mem_prompt'] = jax.random.normal(nk(), (BATCH, N_MEM, D_MODEL), f32)
    inp['w_in_ab'] = w((N_AB, D_MODEL, AB_IN), D_MODEL)
    inp['b_forget'] = jax.random.uniform(nk(), (N_AB, FOX_HEADS), f32, 1.0, 6.0)
    inp['w_pool'] = w((N_AB, POOL_GROUPS, POOL_GC, POOL_GC), POOL_GC)
    inp['pool_scale'] = 1.0 + 0.1 * jax.random.normal(nk(), (N_AB, POOL_WIDTH), f32)
    inp['w_out_ab'] = w((N_AB, AB_OUT, D_MODEL), AB_OUT, BETA)
    inp['w_in_cd'] = w((N_CD, D_MODEL, CD_IN), D_MODEL)
    inp['conv_w'] = w((N_CD, CONV_W, CONV_CH), CONV_W)
    inp['conv_b'] = small((N_CD, CONV_CH))
    dt0 = jnp.exp(jax.random.uniform(nk(), (N_CD, SSM_HEADS), f32, math.log(1e-3), math.log(1e-1)))
    inp['dt_bias'] = dt0 + jnp.log(-jnp.expm1(-dt0))
    inp['a_log'] = jnp.log(jax.random.uniform(nk(), (N_CD, SSM_HEADS), f32, 1.0, 16.0))
    inp['d_skip'] = 1.0 + 0.1 * jax.random.normal(nk(), (N_CD, SSM_HEADS), f32)
    inp['ssm_norm_w'] = gain((N_CD, SSM_INNER))
    inp['gmlp_ln_g'] = gain((N_CD, GMLP_WIDTH))
    inp['gmlp_ln_b'] = small((N_CD, GMLP_WIDTH))
    inp['w_spatial'] = w((N_CD, GMLP_GROUPS, GMLP_CHUNK, GMLP_CHUNK), GMLP_CHUNK)
    inp['b_spatial'] = 1.0 + 0.1 * jax.random.normal(nk(), (N_CD, GMLP_GROUPS, GMLP_CHUNK), f32)
    inp['w_out_cd'] = w((N_CD, CD_OUT, D_MODEL), CD_OUT, BETA)
    inp['ln_mix_g'] = gain((DEPTH, D_MODEL))
    inp['ln_mix_b'] = small((DEPTH, D_MODEL))
    inp['w_xq'] = w((DEPTH, D_MODEL, D_MODEL), D_MODEL)
    inp['w_xk'] = w((DEPTH, D_MODEL, D_MODEL), D_MODEL)
    inp['w_xv'] = w((DEPTH, D_MODEL, D_MODEL), D_MODEL)
    inp['w_xo'] = w((DEPTH, D_MODEL, D_MODEL), D_MODEL, BETA)
    inp['ln_x_g'] = gain((DEPTH, D_MODEL))
    inp['ln_x_b'] = small((DEPTH, D_MODEL))
    inp['w_gate'] = w((DEPTH, D_MODEL, D_FF), D_MODEL)
    inp['w_up'] = w((DEPTH, D_MODEL, D_FF), D_MODEL)
    inp['w_down'] = w((DEPTH, D_FF, D_MODEL), D_FF, BETA)
    inp['ln_ff_g'] = gain((DEPTH, D_MODEL))
    inp['ln_ff_b'] = small((DEPTH, D_MODEL))
    return inp


def reference(x_prompt, x_sample, cache_fox_k, cache_fox_v, cache_fox_logf, state_pool, state_conv,
              state_ssm, cache_mem_k, cache_mem_v, page_table, mem_prompt,
              w_in_ab, b_forget, w_pool, pool_scale, w_out_ab,
              w_in_cd, conv_w, conv_b, dt_bias, a_log, d_skip, ssm_norm_w, gmlp_ln_g, gmlp_ln_b,
              w_spatial, b_spatial, w_out_cd,
              ln_mix_g, ln_mix_b, w_xq, w_xk, w_xv, w_xo, ln_x_g, ln_x_b,
              w_gate, w_up, w_down, ln_ff_g, ln_ff_b):
    yp, ys = x_prompt, x_sample
    bp = x_prompt.shape[0]
    fkp, fvp, flp, fks, fvs, fls = [], [], [], [], [], []
    pps, pss = [], []
    cps, css, hps, hss, gvs = [], [], [], [], []
    mkp, mvp = [], []
    for i in range(DEPTH):
        j = i // 2
        if i % 2 == 0:
            q, k, v, lf, u = _ab_project(yp, w_in_ab[j], b_forget[j])
            att = _fox_prompt(q, k, v, lf)
            pool, pst = _pool_mixer(u, jnp.zeros((bp, POOL_BUF, POOL_WIDTH), u.dtype), 0, w_pool[j], pool_scale[j])
            mix_p = jnp.concatenate([att, pool.astype(att.dtype)], axis=-1) @ w_out_ab[j]
            fkp.append(k)
            fvp.append(v)
            flp.append(lf)
            pps.append(pst)
            q, k, v, lf, u = _ab_project(ys, w_in_ab[j], b_forget[j])
            att = _fox_sample(q, k, v, lf, cache_fox_k[j], cache_fox_v[j], cache_fox_logf[j], page_table)
            pool, pst = _pool_mixer(u, state_pool[j], PAST_LEN, w_pool[j], pool_scale[j])
            mix_s = jnp.concatenate([att, pool.astype(att.dtype)], axis=-1) @ w_out_ab[j]
            fks.append(k)
            fvs.append(v)
            fls.append(lf)
            pss.append(pst)
        else:
            mix_p, cst, hst, _ = _cd_mixer(
                yp, jnp.zeros((bp, CONV_W - 1, CONV_CH), yp.dtype),
                jnp.zeros((bp, SSM_HEADS, SSM_HD, SSM_STATE), jnp.float32),
                w_in_cd[j], conv_w[j], conv_b[j], dt_bias[j], a_log[j], d_skip[j], ssm_norm_w[j],
                gmlp_ln_g[j], gmlp_ln_b[j], w_spatial[j], b_spatial[j], w_out_cd[j])
            cps.append(cst)
            hps.append(hst)
            mix_s, cst, hst, gv = _cd_mixer(
                ys, state_conv[j], state_ssm[j],
                w_in_cd[j], conv_w[j], conv_b[j], dt_bias[j], a_log[j], d_skip[j], ssm_norm_w[j],
                gmlp_ln_g[j], gmlp_ln_b[j], w_spatial[j], b_spatial[j], w_out_cd[j])
            css.append(cst)
            hss.append(hst)
            gvs.append(gv)
        yp = _post(yp, mix_p, ln_mix_g[i], ln_mix_b[i])
        ys = _post(ys, mix_s, ln_mix_g[i], ln_mix_b[i])
        mk, mv = _mem_kv(mem_prompt, w_xk[i], w_xv[i])
        mkp.append(mk)
        mvp.append(mv)
        yp = _post(yp, _cross_attn(yp, mk, mv, w_xq[i], w_xo[i]), ln_x_g[i], ln_x_b[i])
        ys = _post(ys, _cross_attn(ys, cache_mem_k[i], cache_mem_v[i], w_xq[i], w_xo[i]), ln_x_g[i], ln_x_b[i])
        yp = _post(yp, _swiglu(yp, w_gate[i], w_up[i], w_down[i]), ln_ff_g[i], ln_ff_b[i])
        ys = _post(ys, _swiglu(ys, w_gate[i], w_up[i], w_down[i]), ln_ff_g[i], ln_ff_b[i])
    return (yp, ys,
            jnp.stack(fkp), jnp.stack(fvp), jnp.stack(flp),
            jnp.stack(fks), jnp.stack(fvs), jnp.stack(fls),
            jnp.stack(pps), jnp.stack(pss),
            jnp.stack(cps), jnp.stack(css),
            jnp.stack(hps), jnp.stack(hss),
            jnp.stack(gvs),
            jnp.stack(mkp), jnp.stack(mvp))
```

```python
import functools
import math

import numpy as np
import jax
import jax.numpy as jnp
from jax import lax
from jax.experimental import pallas as pl
from jax.experimental.pallas import tpu as pltpu

F32 = jnp.float32
BF16 = jnp.bfloat16

DEPTH = 2
MEM_HEADS = 4
FOX_HEADS = 8
FOX_HD = 64
POOL_WINDOWS = (2, 4, 8, 16)
POOL_GC = 128
POOL_BUF = 15
SSM_HEADS = 16
SSM_HD = 64
SSM_GROUPS = 2
SSM_STATE = 128
SSM_CHUNK = 128
CONV_W = 4
GMLP_GROUPS = 4
GMLP_GC = 128
GMLP_CHUNK = 128
ALPHA = (2 * DEPTH) ** 0.25
LN_EPS = 1e-5

LANES = 128
VMEM_LIMIT_BYTES = 56 * 1024 * 1024


def _cparams(*sem):
    return pltpu.CompilerParams(dimension_semantics=sem, vmem_limit_bytes=VMEM_LIMIT_BYTES)


def _const_spec(shape):
    nd = len(shape)
    return pl.BlockSpec(shape, lambda *_: (0,) * nd, pipeline_mode=pl.Buffered(1))


def _ln_rows(x, g, b):
    mu = jnp.mean(x, axis=-1, keepdims=True)
    xc = x - mu
    var = jnp.mean(xc * xc, axis=-1, keepdims=True)
    return xc * lax.rsqrt(var + LN_EPS) * g + b


def _split3(a):
    hi = a.astype(BF16)
    r = a - hi.astype(F32)
    mid = r.astype(BF16)
    lo = (r - mid.astype(F32)).astype(BF16)
    return hi, mid, lo


def _dot(a, b):
    return jnp.dot(a, b, preferred_element_type=F32)


def _dot_nt(a, b):
    return lax.dot_general(a, b, (((1,), (1,)), ((), ())), preferred_element_type=F32)


def _dot_f32_lhs(a, m):
    hi, mid, lo = _split3(a)
    return _dot(hi, m) + _dot(mid, m) + _dot(lo, m)


def _dot_f32_rhs(m, a):
    hi, mid, lo = _split3(a)
    return _dot(m, hi) + _dot(m, mid) + _dot(m, lo)


def _tril_ones(n, dtype):
    r = lax.broadcasted_iota(jnp.int32, (n, n), 0)
    c = lax.broadcasted_iota(jnp.int32, (n, n), 1)
    return (c <= r).astype(dtype)


def _softplus(x):
    return jnp.maximum(x, 0.0) + jnp.log1p(jnp.exp(-jnp.abs(x)))


def _log_sigmoid(x):
    return jnp.minimum(x, 0.0) - jnp.log1p(jnp.exp(-jnp.abs(x)))


def _silu(x):
    return x / (1.0 + jnp.exp(-x))


def _gelu_tanh(x):
    c = math.sqrt(2.0 / math.pi)
    return 0.5 * x * (1.0 + jnp.tanh(c * (x + 0.044715 * (x * x * x))))


def _ab_proj_kernel(x_ref, w_ref, bf_ref, place_ref, wpool_ref, pscale_ref,
                    qa_ref, ka_ref, va_ref, k_ref, v_ref, lf_ref, pool_ref, tail_ref,
                    carry_ref, ext_ref, *, tm, tiles_per_seq):
    i = pl.program_id(0)
    first = (i % tiles_per_seq) == 0
    hw = FOX_HEADS * LANES
    fw = FOX_HEADS * FOX_HD
    pw = POOL_GC * len(POOL_WINDOWS)

    @pl.when(first)
    def _():
        carry_ref[...] = jnp.zeros_like(carry_ref)
        ext_ref[0:16, :] = jnp.zeros((16, pw), F32)

    xb = x_ref[...].astype(BF16)
    o = 0
    qa = _dot(xb, w_ref[:, o:o + hw]) * (FOX_HD ** -0.5); o += hw
    ka = _dot(xb, w_ref[:, o:o + hw]); o += hw
    va = _dot(xb, w_ref[:, o:o + hw]); o += hw
    k = _dot(xb, w_ref[:, o:o + fw]); o += fw
    v = _dot(xb, w_ref[:, o:o + fw]); o += fw
    u = _dot(xb, w_ref[:, o:o + pw]); o += pw
    fl = _dot(xb, w_ref[:, o:o + LANES])

    k_ref[...] = k
    v_ref[...] = v
    va_ref[...] = va.astype(BF16)

    lf = _log_sigmoid(fl + bf_ref[...])
    lf_ref[...] = lf[:, :FOX_HEADS]
    c = _dot_f32_rhs(_tril_ones(tm, BF16), lf) + carry_ref[...]
    carry_ref[...] = c[tm - 1:tm, :]
    chi, cmid, clo = _split3(c)
    ones_q = place_ref[6, 0:1, :].astype(F32)
    ones_k = place_ref[7, 0:1, :].astype(F32)
    qa = qa + _dot(chi, place_ref[0]) + _dot(cmid, place_ref[1]) + _dot(clo, place_ref[2]) + ones_q
    ka = ka - _dot(chi, place_ref[3]) - _dot(cmid, place_ref[4]) - _dot(clo, place_ref[5]) + ones_k
    qa_ref[...] = qa.astype(BF16)
    ka_ref[...] = ka.astype(BF16)

    ext_ref[16:16 + tm, :] = u
    pos = (i % tiles_per_seq) * tm + lax.broadcasted_iota(jnp.int32, (tm, 1), 0)
    for g, w in enumerate(POOL_WINDOWS):
        sl = slice(g * POOL_GC, (g + 1) * POOL_GC)
        s = ext_ref[16:16 + tm, sl]
        for j in range(1, w):
            s = s + ext_ref[16 - j:16 - j + tm, sl]
        cnt = jnp.minimum(pos + 1, w).astype(F32)
        dlt = s / cnt - ext_ref[16:16 + tm, sl]
        y = _dot(dlt.astype(BF16), wpool_ref[g]) * pscale_ref[:, sl]
        pool_ref[:, sl] = y.astype(BF16)
    tail = ext_ref[tm:tm + 16, :]
    tail_ref[0] = tail
    ext_ref[0:16, :] = tail


def _ab_weights(w_in, b_forget):
    d = w_in.shape[0]
    fw = FOX_HEADS * FOX_HD

    def head_pad(w):
        w = w.reshape(d, FOX_HEADS, FOX_HD)
        return jnp.pad(w, ((0, 0), (0, 0), (0, LANES - FOX_HD))).reshape(d, FOX_HEADS * LANES)

    wq, wk, wv = w_in[:, :fw], w_in[:, fw:2 * fw], w_in[:, 2 * fw:3 * fw]
    wf = jnp.pad(w_in[:, 3 * fw:3 * fw + FOX_HEADS], ((0, 0), (0, LANES - FOX_HEADS)))
    wu = w_in[:, 3 * fw + FOX_HEADS:]
    w_all = jnp.concatenate([head_pad(wq), head_pad(wk), head_pad(wv), wk, wv, wu, wf], axis=1).astype(BF16)
    bf = jnp.pad(b_forget, (0, LANES - FOX_HEADS)).reshape(1, LANES).astype(F32)
    return w_all, bf


def _fox_placement():
    hw = FOX_HEADS * LANES
    p = np.zeros((8, LANES, hw), np.float32)
    for h in range(FOX_HEADS):
        for j in range(3):
            p[j, h, h * LANES + FOX_HD + j] = 1.0
            p[3 + j, h, h * LANES + FOX_HD + 3 + j] = 1.0
            p[6, :, h * LANES + FOX_HD + 3 + j] = 1.0
            p[7, :, h * LANES + FOX_HD + j] = 1.0
    return jnp.asarray(p, BF16)


def _ab_proj_prompt(x2, w_all, bf, place, w_pool, pool_scale, *, batch, tm):
    n, d = x2.shape
    seq = n // batch
    tiles_per_seq = seq // tm
    hw = FOX_HEADS * LANES
    fw = FOX_HEADS * FOX_HD
    pw = POOL_GC * len(POOL_WINDOWS)
    row = lambda w: pl.BlockSpec((tm, w), lambda i: (i, 0))
    kern = functools.partial(_ab_proj_kernel, tm=tm, tiles_per_seq=tiles_per_seq)
    return pl.pallas_call(
        kern,
        grid=(n // tm,),
        in_specs=[row(d), _const_spec(w_all.shape), _const_spec(bf.shape), _const_spec(place.shape),
                  _const_spec(w_pool.shape), _const_spec(pool_scale.shape)],
        out_specs=[row(hw), row(hw), row(hw), row(fw), row(fw),
                   pl.BlockSpec((tm, FOX_HEADS), lambda i: (i, 0)), row(pw),
                   pl.BlockSpec((1, 16, pw), lambda i: (i // tiles_per_seq, 0, 0))],
        out_shape=[jax.ShapeDtypeStruct((n, hw), BF16), jax.ShapeDtypeStruct((n, hw), BF16),
                   jax.ShapeDtypeStruct((n, hw), BF16), jax.ShapeDtypeStruct((n, fw), F32),
                   jax.ShapeDtypeStruct((n, fw), F32), jax.ShapeDtypeStruct((n, FOX_HEADS), F32),
                   jax.ShapeDtypeStruct((n, pw), BF16), jax.ShapeDtypeStruct((batch, 16, pw), F32)],
        scratch_shapes=[pltpu.VMEM((1, LANES), F32), pltpu.VMEM((tm + 16, pw), F32)],
        compiler_params=_cparams("arbitrary"),
        name="ab_proj_prompt",
    )(x2, w_all, bf, place, w_pool, pool_scale)


def _fox_attn_kernel(qi_ref, ki_ref, q_ref, k_ref, v_ref, o_ref, m_ref, l_ref, acc_ref, *, tq, tk):
    t = pl.program_id(2)
    qi = qi_ref[t]
    ki = ki_ref[t]

    @pl.when(ki == 0)
    def _():
        m_ref[...] = jnp.full_like(m_ref, -jnp.inf)
        l_ref[...] = jnp.zeros_like(l_ref)
        acc_ref[...] = jnp.zeros_like(acc_ref)

    def step(masked):
        for hh in range(2):
            sl = slice(hh * LANES, (hh + 1) * LANES)
            s = _dot_nt(q_ref[:, sl], k_ref[:, sl])
            if masked:
                qpos = qi * tq + lax.broadcasted_iota(jnp.int32, (tq, tk), 0)
                kpos = ki * tk + lax.broadcasted_iota(jnp.int32, (tq, tk), 1)
                s = jnp.where(kpos <= qpos, s, -jnp.inf)
            m_prev = m_ref[hh]
            m_new = jnp.maximum(m_prev, jnp.max(s, axis=-1, keepdims=True))
            a = jnp.exp(m_prev - m_new)
            p = jnp.exp(s - m_new)
            l_ref[hh] = a * l_ref[hh] + jnp.sum(p, axis=-1, keepdims=True)
            acc_ref[hh] = a * acc_ref[hh] + _dot(p.astype(BF16), v_ref[:, sl])
            m_ref[hh] = m_new

    needs_mask = (ki + 1) * tk - 1 > qi * tq

    @pl.when(needs_mask)
    def _():
        step(True)

    @pl.when(jnp.logical_not(needs_mask))
    def _():
        step(False)

    last = (ki + 1) * tk >= (qi + 1) * tq

    @pl.when(last)
    def _():
        for hh in range(2):
            out = acc_ref[hh] / l_ref[hh]
            o_ref[:, hh * FOX_HD:(hh + 1) * FOX_HD] = out[:, :FOX_HD].astype(o_ref.dtype)


def _fox_attn_prompt(qa, ka, va, *, batch, tq, tk):
    n = qa.shape[0]
    seq = n // batch
    nq, nk = seq // tq, seq // tk
    pairs = [(a, b) for a in range(nq) for b in range(nk) if b * tk <= a * tq + tq - 1]
    qi_tab = jnp.asarray([p[0] for p in pairs], jnp.int32)
    ki_tab = jnp.asarray([p[1] for p in pairs], jnp.int32)
    hp = FOX_HEADS // 2
    grid_spec = pltpu.PrefetchScalarGridSpec(
        num_scalar_prefetch=2,
        grid=(batch, hp, len(pairs)),
        in_specs=[pl.BlockSpec((tq, 2 * LANES), lambda b, h, t, qt, kt: (b * nq + qt[t], h)),
                  pl.BlockSpec((tk, 2 * LANES), lambda b, h, t, qt, kt: (b * nk + kt[t], h)),
                  pl.BlockSpec((tk, 2 * LANES), lambda b, h, t, qt, kt: (b * nk + kt[t], h))],
        out_specs=pl.BlockSpec((tq, 2 * FOX_HD), lambda b, h, t, qt, kt: (b * nq + qt[t], h)),
        scratch_shapes=[pltpu.VMEM((2, tq, 1), F32), pltpu.VMEM((2, tq, 1), F32),
                        pltpu.VMEM((2, tq, LANES), F32)],
    )
    return pl.pallas_call(
        functools.partial(_fox_attn_kernel, tq=tq, tk=tk),
        grid_spec=grid_spec,
        out_shape=jax.ShapeDtypeStruct((n, FOX_HEADS * FOX_HD), BF16),
        compiler_params=_cparams("parallel", "parallel", "arbitrary"),
        name="fox_attn_prompt",
    )(qi_tab, ki_tab, qa, ka, va)


def _proj_ln_kernel(*refs, n_pieces):
    x_ref = refs[0]
    a_refs = refs[1:1 + n_pieces]
    w_refs = refs[1 + n_pieces:1 + 2 * n_pieces]
    g_ref, b_ref, o_ref = refs[1 + 2 * n_pieces:]
    sub = _dot(a_refs[0][...].astype(BF16), w_refs[0][...])
    for a_ref, w_ref in zip(a_refs[1:], w_refs[1:]):
        sub = sub + _dot(a_ref[...].astype(BF16), w_ref[...])
    o_ref[...] = _ln_rows(ALPHA * x_ref[...] + sub, g_ref[...], b_ref[...])


def _proj_ln(x2, pieces, weights, g, b, *, tm):
    n, d = x2.shape
    row = lambda w: pl.BlockSpec((tm, w), lambda i: (i, 0))
    return pl.pallas_call(
        functools.partial(_proj_ln_kernel, n_pieces=len(pieces)),
        grid=(n // tm,),
        in_specs=[row(d)] + [row(p.shape[1]) for p in pieces] + [_const_spec(w.shape) for w in weights]
                 + [_const_spec(g.shape), _const_spec(b.shape)],
        out_specs=row(d),
        out_shape=jax.ShapeDtypeStruct((n, d), F32),
        compiler_params=_cparams("parallel"),
        name="proj_ln",
    )(x2, *pieces, *weights, g, b)


def _mem_kv_kernel(m_ref, wk_ref, wv_ref, k_ref, v_ref, kb_ref, vb_ref):
    mb = m_ref[...].astype(BF16)
    k = _dot(mb, wk_ref[...])
    v = _dot(mb, wv_ref[...])
    k_ref[...] = k
    v_ref[...] = v
    kb_ref[...] = k.astype(BF16)
    vb_ref[...] = v.astype(BF16)


def _mem_kv(mem2, wk, wv, *, tm):
    n, d = mem2.shape
    row = pl.BlockSpec((tm, d), lambda i: (i, 0))
    return pl.pallas_call(
        _mem_kv_kernel,
        grid=(n // tm,),
        in_specs=[row, _const_spec(wk.shape), _const_spec(wv.shape)],
        out_specs=[row, row, row, row],
        out_shape=[jax.ShapeDtypeStruct((n, d), F32), jax.ShapeDtypeStruct((n, d), F32),
                   jax.ShapeDtypeStruct((n, d), BF16), jax.ShapeDtypeStruct((n, d), BF16)],
        compiler_params=_cparams("parallel"),
        name="mem_kv",
    )(mem2, wk, wv)


def _cross_attn_kernel(x_ref, wq_ref, mk_ref, mv_ref, wo_ref, g_ref, b_ref, o_ref, *, hd):
    x = x_ref[...]
    q = (_dot(x.astype(BF16), wq_ref[...]) * (hd ** -0.5)).astype(BF16)
    outs = []
    for h in range(MEM_HEADS):
        sl = slice(h * hd, (h + 1) * hd)
        s = _dot_nt(q[:, sl], mk_ref[:, sl])
        e = jnp.exp(s - jnp.max(s, axis=-1, keepdims=True))
        p = e / jnp.sum(e, axis=-1, keepdims=True)
        outs.append(_dot(p.astype(BF16), mv_ref[:, sl]).astype(BF16))
    o = jnp.concatenate(outs, axis=-1)
    o_ref[...] = _ln_rows(ALPHA * x + _dot(o, wo_ref[...]), g_ref[...], b_ref[...])


def _cross_attn_prompt(x2, wq, mk_b, mv_b, wo, g, b, *, batch, tm):
    n, d = x2.shape
    n_mem = mk_b.shape[0] // batch
    tiles_per_seq = (n // batch) // tm
    row = pl.BlockSpec((tm, d), lambda i: (i, 0))
    mem = pl.BlockSpec((n_mem, d), lambda i: (i // tiles_per_seq, 0))
    return pl.pallas_call(
        functools.partial(_cross_attn_kernel, hd=d // MEM_HEADS),
        grid=(n // tm,),
        in_specs=[row, _const_spec(wq.shape), mem, mem, _const_spec(wo.shape),
                  _const_spec(g.shape), _const_spec(b.shape)],
        out_specs=row,
        out_shape=jax.ShapeDtypeStruct((n, d), F32),
        compiler_params=_cparams("parallel"),
        name="cross_attn_prompt",
    )(x2, wq, mk_b, mv_b, wo, g, b)


def _swiglu_kernel(x_ref, wg_ref, wu_ref, wd_ref, g_ref, b_ref, o_ref, *, chunk):
    x = x_ref[...]
    xb = x.astype(BF16)
    dff = wg_ref.shape[1]
    acc = None
    for c in range(dff // chunk):
        sl = slice(c * chunk, (c + 1) * chunk)
        hg = _dot(xb, wg_ref[:, sl])
        hu = _dot(xb, wu_ref[:, sl])
        part = _dot((_silu(hg) * hu).astype(BF16), wd_ref[sl, :])
        acc = part if acc is None else acc + part
    o_ref[...] = _ln_rows(ALPHA * x + acc, g_ref[...], b_ref[...])


def _swiglu(x2, wg, wu, wd, g, b, *, tm):
    n, d = x2.shape
    row = pl.BlockSpec((tm, d), lambda i: (i, 0))
    return pl.pallas_call(
        functools.partial(_swiglu_kernel, chunk=256),
        grid=(n // tm,),
        in_specs=[row, _const_spec(wg.shape), _const_spec(wu.shape), _const_spec(wd.shape),
                  _const_spec(g.shape), _const_spec(b.shape)],
        out_specs=row,
        out_shape=jax.ShapeDtypeStruct((n, d), F32),
        compiler_params=_cparams("parallel"),
        name="swiglu",
    )(x2, wg, wu, wd, g, b)


def _cd_proj_kernel(x_ref, w_ref, cw_ref, cb_ref, dtb_ref, aneg_ref, lng_ref, lnb_ref, wsp_ref, bsp_ref,
                    z_ref, xs_ref, bm_ref, cm_ref, dt_ref, da_ref, yd_ref, tail_ref,
                    ext_ref, *, tm, tiles_per_seq):
    i = pl.program_id(0)
    inner = SSM_HEADS * SSM_HD
    bcw = SSM_GROUPS * SSM_STATE
    cch = inner + 2 * bcw
    gw = GMLP_GROUPS * GMLP_GC

    @pl.when((i % tiles_per_seq) == 0)
    def _():
        ext_ref[0:8, :] = jnp.zeros((8, cch), F32)

    xb = x_ref[...].astype(BF16)
    o = 0
    z_ref[...] = _dot(xb, w_ref[:, o:o + inner]); o += inner
    ext_ref[8:8 + tm, :] = _dot(xb, w_ref[:, o:o + cch]); o += cch
    dt_raw = _dot(xb, w_ref[:, o:o + LANES]); o += LANES
    gu = _dot(xb, w_ref[:, o:o + gw]); o += gw
    gv = _dot(xb, w_ref[:, o:o + gw])

    conv = cb_ref[...] + cw_ref[CONV_W - 1:CONV_W, :] * ext_ref[8:8 + tm, :]
    for j in range(CONV_W - 1):
        off = 8 - (CONV_W - 1) + j
        conv = conv + cw_ref[j:j + 1, :] * ext_ref[off:off + tm, :]
    act = _silu(conv)
    xs_ref[...] = act[:, :inner]
    bm_ref[...] = act[:, inner:inner + bcw].astype(BF16)
    cm_ref[...] = act[:, inner + bcw:].astype(BF16)
    tail = ext_ref[tm:tm + 8, :]
    tail_ref[0] = tail
    ext_ref[0:8, :] = tail

    dt = _softplus(dt_raw + dtb_ref[...])
    dt_ref[...] = dt
    da_ref[...] = dt * aneg_ref[...]

    u = _gelu_tanh(gu)
    v = _ln_rows(_gelu_tanh(gv), lng_ref[...], lnb_ref[...]).astype(BF16)
    tri = _tril_ones(GMLP_CHUNK, F32)
    for g in range(GMLP_GROUPS):
        wm = (wsp_ref[g] * tri).astype(BF16)
        sl = slice(g * GMLP_GC, (g + 1) * GMLP_GC)
        for c in range(tm // GMLP_CHUNK):
            rows = slice(c * GMLP_CHUNK, (c + 1) * GMLP_CHUNK)
            sg = _dot(wm, v[rows, sl]) + bsp_ref[:, sl]
            yd_ref[rows, sl] = (u[rows, sl] * sg).astype(BF16)


def _cd_weights(w_in, dt_bias, a_log):
    inner = SSM_HEADS * SSM_HD
    cch = inner + 2 * SSM_GROUPS * SSM_STATE
    o2 = inner + cch
    o3 = o2 + SSM_HEADS
    pad = LANES - SSM_HEADS
    w_dt = jnp.pad(w_in[:, o2:o3], ((0, 0), (0, pad)))
    w_all = jnp.concatenate([w_in[:, :o2], w_dt, w_in[:, o3:]], axis=1).astype(BF16)
    dtb = jnp.pad(dt_bias, (0, pad)).reshape(1, LANES).astype(F32)
    a_neg = jnp.pad(-jnp.exp(a_log.astype(F32)), (0, pad)).reshape(1, LANES)
    return w_all, dtb, a_neg


def _cd_proj_prompt(x2, w_all, conv_w, conv_b, dtb, a_neg, ln_g, ln_b, w_sp, bsp_full, *, batch, tm):
    n, d = x2.shape
    tiles_per_seq = (n // batch) // tm
    inner = SSM_HEADS * SSM_HD
    bcw = SSM_GROUPS * SSM_STATE
    cch = inner + 2 * bcw
    gw = GMLP_GROUPS * GMLP_GC
    row = lambda w: pl.BlockSpec((tm, w), lambda i: (i, 0))
    consts = [w_all, conv_w, conv_b, dtb, a_neg, ln_g, ln_b, w_sp, bsp_full]
    return pl.pallas_call(
        functools.partial(_cd_proj_kernel, tm=tm, tiles_per_seq=tiles_per_seq),
        grid=(n // tm,),
        in_specs=[row(d)] + [_const_spec(c.shape) for c in consts],
        out_specs=[row(inner), row(inner), row(bcw), row(bcw), row(LANES), row(LANES), row(gw),
                   pl.BlockSpec((1, 8, cch), lambda i: (i // tiles_per_seq, 0, 0))],
        out_shape=[jax.ShapeDtypeStruct((n, inner), F32), jax.ShapeDtypeStruct((n, inner), F32),
                   jax.ShapeDtypeStruct((n, bcw), BF16), jax.ShapeDtypeStruct((n, bcw), BF16),
                   jax.ShapeDtypeStruct((n, LANES), F32), jax.ShapeDtypeStruct((n, LANES), F32),
                   jax.ShapeDtypeStruct((n, gw), BF16), jax.ShapeDtypeStruct((batch, 8, cch), F32)],
        scratch_shapes=[pltpu.VMEM((tm + 8, cch), F32)],
        compiler_params=_cparams("arbitrary"),
        name="cd_proj_prompt",
    )(x2, *consts)


def _ssd_kernel(xs_ref, bm_ref, cm_ref, dt_ref, da_ref, z_ref, e_ref, dskip_ref, nw_ref,
                y_ref, hlast_ref, st_ref, *, n_chunks, steps_per_seq):
    j = pl.program_id(1)
    lc = SSM_CHUNK
    inner = SSM_HEADS * SSM_HD
    gwid = inner // SSM_GROUPS

    @pl.when(j == 0)
    def _():
        st_ref[...] = jnp.zeros_like(st_ref)

    tri_b = _tril_ones(lc, BF16)
    causal = _tril_ones(lc, F32) > 0.5
    lane = lax.broadcasted_iota(jnp.int32, (lc, LANES), 1)
    expand = e_ref[...]
    for c in range(n_chunks):
        rows = slice(c * lc, (c + 1) * lc)
        xs = xs_ref[rows, :]
        acs = _dot_f32_rhs(tri_b, da_ref[rows, :])
        acs_t = acs.T
        xdt = xs * _dot_f32_lhs(dt_ref[rows, :], expand)
        ea_x = _dot_f32_lhs(jnp.exp(acs), expand)
        dte_x = _dot_f32_lhs(jnp.exp(acs[lc - 1:lc, :] - acs), expand)
        xdt_b = xdt.astype(BF16)
        xd_b = (xdt * dte_x).astype(BF16)
        y_parts = []
        for g in range(SSM_GROUPS):
            gs = slice(g * SSM_STATE, (g + 1) * SSM_STATE)
            cs = slice(g * gwid, (g + 1) * gwid)
            bg = bm_ref[rows, gs]
            cg = cm_ref[rows, gs]
            cb = _dot_nt(cg, bg)
            y_off = _dot(cg, st_ref[:, cs].astype(BF16)) * ea_x[:, cs]
            pieces = []
            for hp in range(SSM_HEADS // SSM_GROUPS // 2):
                ms = []
                for hh in range(2):
                    h = g * (SSM_HEADS // SSM_GROUPS) + hp * 2 + hh
                    seg = jnp.exp(jnp.where(causal, acs[:, h:h + 1] - acs_t[h:h + 1, :], -jnp.inf))
                    ms.append((cb * seg).astype(BF16))
                col0 = g * gwid + hp * LANES
                xp = xdt_b[:, col0:col0 + LANES]
                zero = jnp.zeros_like(xp)
                x_stack = jnp.concatenate([jnp.where(lane < SSM_HD, xp, zero),
                                           jnp.where(lane >= SSM_HD, xp, zero)], axis=0)
                pieces.append(_dot(jnp.concatenate(ms, axis=1), x_stack))
            y_parts.append(jnp.concatenate(pieces, axis=1) + y_off)
            bg_t = bg.astype(F32).T.astype(BF16)
            st_ref[:, cs] = st_ref[:, cs] * ea_x[lc - 1:lc, cs] + _dot(bg_t, xd_b[:, cs])
        y = jnp.concatenate(y_parts, axis=1) + dskip_ref[...] * xs
        gated = y * _silu(z_ref[rows, :])
        ms2 = jnp.mean(gated * gated, axis=-1, keepdims=True)
        y_ref[rows, :] = (gated * lax.rsqrt(ms2 + LN_EPS) * nw_ref[...]).astype(y_ref.dtype)

    @pl.when(j == steps_per_seq - 1)
    def _():
        hlast_ref[0] = st_ref[...].T


def _ssd_expand():
    e = np.zeros((LANES, SSM_HEADS * SSM_HD), np.float32)
    for h in range(SSM_HEADS):
        e[h, h * SSM_HD:(h + 1) * SSM_HD] = 1.0
    return jnp.asarray(e, BF16)


def _ssd_prompt(xs, bm, cm, dt, da, z, expand, dskip_row, norm_w, *, batch, tr):
    n, inner = xs.shape
    steps_per_seq = (n // batch) // tr
    bcw = bm.shape[1]
    row = lambda w: pl.BlockSpec((tr, w), lambda b, j: (b * steps_per_seq + j, 0))
    const = lambda a: pl.BlockSpec(a.shape, lambda b, j: (0,) * a.ndim, pipeline_mode=pl.Buffered(1))
    return pl.pallas_call(
        functools.partial(_ssd_kernel, n_chunks=tr // SSM_CHUNK, steps_per_seq=steps_per_seq),
        grid=(batch, steps_per_seq),
        in_specs=[row(inner), row(bcw), row(bcw), row(LANES), row(LANES), row(inner),
                  const(expand), const(dskip_row), const(norm_w)],
        out_specs=[row(inner), pl.BlockSpec((1, inner, SSM_STATE), lambda b, j: (b, 0, 0))],
        out_shape=[jax.ShapeDtypeStruct((n, inner), BF16),
                   jax.ShapeDtypeStruct((batch, inner, SSM_STATE), F32)],
        scratch_shapes=[pltpu.VMEM((SSM_STATE, inner), F32)],
        compiler_params=_cparams("parallel", "arbitrary"),
        name="ssd_prompt",
    )(xs, bm, cm, dt, da, z, expand, dskip_row, norm_w)


def _j_ln(x, g, b):
    mu = jnp.mean(x, axis=-1, keepdims=True)
    var = jnp.mean(jnp.square(x - mu), axis=-1, keepdims=True)
    return (x - mu) * lax.rsqrt(var + LN_EPS) * g + b


def _j_fox_sample(q, k_new, v_new, logf_new, k_pool, v_pool, lf_pool, page_table):
    b, l, h, d = q.shape
    past = page_table.shape[1] * k_pool.shape[1]
    k_all = jnp.concatenate([k_pool[page_table].reshape(b, past, h, d), k_new], axis=1)
    v_all = jnp.concatenate([v_pool[page_table].reshape(b, past, h, d), v_new], axis=1)
    lf_all = jnp.concatenate([lf_pool[page_table].reshape(b, past, h), logf_new], axis=1)
    c = jnp.cumsum(lf_all, axis=1).transpose(0, 2, 1)
    sc = jnp.einsum('bqhd,bkhd->bhqk', q, k_all) * (FOX_HD ** -0.5)
    sc = sc + c[:, :, past:, None] - c[:, :, None, :]
    p = jax.nn.softmax(sc, axis=-1)
    return jnp.einsum('bhqk,bkhd->bqhd', p, v_all).reshape(b, l, h * d)


def _j_pool_sample(u, prefix, w_pool, pool_scale):
    b, l, ch = u.shape
    ext = jnp.concatenate([prefix, u], axis=1)
    outs = []
    for g, w in enumerate(POOL_WINDOWS):
        sl = slice(g * POOL_GC, (g + 1) * POOL_GC)
        outs.append(jnp.sum(ext[:, -w:, sl], axis=1, keepdims=True) / float(w) - u[..., sl])
    dlt = jnp.stack(outs, axis=2)
    y = jnp.einsum('blgc,gce->blge', dlt, w_pool).reshape(b, l, ch) * pool_scale
    return y, ext[:, -POOL_BUF:]


def _j_cross_attn(x, mk, mv, wq, wo):
    b, l, d = x.shape
    hd = d // MEM_HEADS
    q = (x @ wq).reshape(b, l, MEM_HEADS, hd)
    s = jnp.einsum('bqhd,bkhd->bhqk', q, mk) * (hd ** -0.5)
    p = jax.nn.softmax(s, axis=-1)
    o = jnp.einsum('bhqk,bkhd->bqhd', p, mv).reshape(b, l, d)
    return o @ wo


def _j_cd_sample(x, conv_prefix, h0, w_in, conv_w, conv_b, dt_bias, a_log, d_skip, norm_w,
                 ln_g, ln_b, w_sp, b_sp, w_out):
    b, l, _ = x.shape
    inner = SSM_HEADS * SSM_HD
    cch = inner + 2 * SSM_GROUPS * SSM_STATE
    proj = x @ w_in
    o1, o2 = inner, inner + cch
    o3 = o2 + SSM_HEADS
    o4 = o3 + GMLP_GROUPS * GMLP_GC
    z, xbc, dt_raw, gu, gv = jnp.split(proj, [o1, o2, o3, o4], axis=-1)
    ext = jnp.concatenate([conv_prefix, xbc], axis=1)
    conv = jnp.einsum('bwc,wc->bc', ext, conv_w)[:, None, :] + conv_b
    conv_state = ext[:, -(CONV_W - 1):]
    xbc = jax.nn.silu(conv)
    xs = xbc[..., :inner].reshape(b, SSM_HEADS, SSM_HD)
    bm = xbc[..., inner:inner + SSM_GROUPS * SSM_STATE].reshape(b, SSM_GROUPS, SSM_STATE)
    cm = xbc[..., inner + SSM_GROUPS * SSM_STATE:].reshape(b, SSM_GROUPS, SSM_STATE)
    dt = jax.nn.softplus(dt_raw[:, 0] + dt_bias)
    da = dt * (-jnp.exp(a_log))
    nr = SSM_HEADS // SSM_GROUPS
    xdt = (xs * dt[..., None]).reshape(b, SSM_GROUPS, nr, SSM_HD)
    h0g = h0.reshape(b, SSM_GROUPS, nr, SSM_HD, SSM_STATE)
    dec = jnp.exp(da).reshape(b, SSM_GROUPS, nr)
    h1 = h0g * dec[..., None, None] + jnp.einsum('bgrp,bgn->bgrpn', xdt, bm)
    y = jnp.einsum('bgn,bgrpn->bgrp', cm, h0g) * dec[..., None] + jnp.einsum('bgn,bgn->bg', cm, bm)[..., None, None] * xdt
    y = y.reshape(b, SSM_HEADS, SSM_HD) + d_skip[:, None] * xs
    y = y.reshape(b, 1, inner) * jax.nn.silu(z)
    y = y * lax.rsqrt(jnp.mean(jnp.square(y), axis=-1, keepdims=True) + LN_EPS) * norm_w
    u = jax.nn.gelu(gu)
    v = _j_ln(jax.nn.gelu(gv), ln_g, ln_b)
    gscale = jnp.repeat(w_sp[:, 0, 0], GMLP_GC)
    gbias = jnp.repeat(b_sp[:, 0], GMLP_GC)
    yd = u * (v * gscale + gbias)
    out = jnp.concatenate([y, yd], axis=-1) @ w_out
    return out, conv_state, h1.reshape(b, SSM_HEADS, SSM_HD, SSM_STATE), v


def kernel(x_prompt, x_sample, cache_fox_k, cache_fox_v, cache_fox_logf, state_pool, state_conv, state_ssm, cache_mem_k, cache_mem_v, page_table, mem_prompt, w_in_ab, b_forget, w_pool, pool_scale, w_out_ab, w_in_cd, conv_w, conv_b, dt_bias, a_log, d_skip, ssm_norm_w, gmlp_ln_g, gmlp_ln_b, w_spatial, b_spatial, w_out_cd, ln_mix_g, ln_mix_b, w_xq, w_xk, w_xv, w_xo, ln_x_g, ln_x_b, w_gate, w_up, w_down, ln_ff_g, ln_ff_b):
    bp, seq, d = x_prompt.shape
    bs = x_sample.shape[0]
    n = bp * seq
    n_mem = mem_prompt.shape[1]
    tm = min(512, seq)
    fw = FOX_HEADS * FOX_HD
    inner = SSM_HEADS * SSM_HD
    row1 = lambda a: a.reshape(1, -1).astype(F32)

    yp = x_prompt.reshape(n, d)
    ys = x_sample
    mem2 = mem_prompt.reshape(bp * n_mem, d)
    place = _fox_placement()
    expand = _ssd_expand()

    outs = {}
    mkp, mvp = [], []
    for i in range(DEPTH):
        j = i // 2
        if i % 2 == 0:
            w_all, bf = _ab_weights(w_in_ab[j], b_forget[j])
            qa, ka, va, k, v, lf, pool, tail = _ab_proj_prompt(
                yp, w_all, bf, place, w_pool[j].astype(BF16), row1(pool_scale[j]), batch=bp, tm=tm)
            att = _fox_attn_prompt(qa, ka, va, batch=bp, tq=tm, tk=tm)
            w_out = w_out_ab[j].astype(BF16)
            mix_p = (att, pool), (w_out[:fw], w_out[fw:])
            outs['fkp'] = k.reshape(1, bp, seq, FOX_HEADS, FOX_HD)
            outs['fvp'] = v.reshape(1, bp, seq, FOX_HEADS, FOX_HD)
            outs['flp'] = lf.reshape(1, bp, seq, FOX_HEADS)
            outs['pps'] = tail[None, :, 1:, :]
            proj = ys @ w_in_ab[j]
            q_s, k_s, v_s, fl_s, u_s = jnp.split(proj, [fw, 2 * fw, 3 * fw, 3 * fw + FOX_HEADS], axis=-1)
            lf_s = jax.nn.log_sigmoid(fl_s + b_forget[j])
            shp = (bs, 1, FOX_HEADS, FOX_HD)
            att_s = _j_fox_sample(q_s.reshape(shp), k_s.reshape(shp), v_s.reshape(shp), lf_s,
                                  cache_fox_k[j], cache_fox_v[j], cache_fox_logf[j], page_table)
            pool_s, pst = _j_pool_sample(u_s, state_pool[j], w_pool[j], pool_scale[j])
            mix_s = jnp.concatenate([att_s, pool_s], axis=-1) @ w_out_ab[j]
            outs['fks'] = k_s.reshape((1,) + shp)
            outs['fvs'] = v_s.reshape((1,) + shp)
            outs['fls'] = lf_s[None]
            outs['pss'] = pst[None]
        else:
            w_all, dtb, a_neg = _cd_weights(w_in_cd[j], dt_bias[j], a_log[j])
            bsp_full = jnp.repeat(b_spatial[j].T, GMLP_GC, axis=1).astype(F32)
            z, xs, bm, cm, dt, da, yd, tail = _cd_proj_prompt(
                yp, w_all, conv_w[j].astype(F32), row1(conv_b[j]), dtb, a_neg,
                row1(gmlp_ln_g[j]), row1(gmlp_ln_b[j]), w_spatial[j].astype(F32), bsp_full, batch=bp, tm=tm)
            dskip_row = row1(jnp.repeat(d_skip[j], SSM_HD))
            y_n, h_last = _ssd_prompt(xs, bm, cm, dt, da, z, expand, dskip_row, row1(ssm_norm_w[j]),
                                      batch=bp, tr=tm)
            w_out = w_out_cd[j].astype(BF16)
            mix_p = (y_n, yd), (w_out[:inner], w_out[inner:])
            outs['cps'] = tail[None, :, 8 - (CONV_W - 1):, :]
            outs['hps'] = h_last.reshape(1, bp, SSM_HEADS, SSM_HD, SSM_STATE)
            mix_s, cst, hst, gv = _j_cd_sample(
                ys, state_conv[j], state_ssm[j], w_in_cd[j], conv_w[j], conv_b[j], dt_bias[j], a_log[j],
                d_skip[j], ssm_norm_w[j], gmlp_ln_g[j], gmlp_ln_b[j], w_spatial[j], b_spatial[j], w_out_cd[j])
            outs['css'] = cst[None]
            outs['hss'] = hst[None]
            outs['gvs'] = gv[None]
        yp = _proj_ln(yp, mix_p[0], mix_p[1], row1(ln_mix_g[i]), row1(ln_mix_b[i]), tm=tm)
        ys = _j_ln(ALPHA * ys + mix_s, ln_mix_g[i], ln_mix_b[i])
        mk, mv, mk_b, mv_b = _mem_kv(mem2, w_xk[i].astype(BF16), w_xv[i].astype(BF16), tm=n_mem)
        mkp.append(mk.reshape(bp, n_mem, MEM_HEADS, d // MEM_HEADS))
        mvp.append(mv.reshape(bp, n_mem, MEM_HEADS, d // MEM_HEADS))
        wq_b, wo_b = w_xq[i].astype(BF16), w_xo[i].astype(BF16)
        yp = _cross_attn_prompt(yp, wq_b, mk_b, mv_b, wo_b, row1(ln_x_g[i]), row1(ln_x_b[i]), batch=bp, tm=tm)
        ys = _j_ln(ALPHA * ys + _j_cross_attn(ys, cache_mem_k[i], cache_mem_v[i], w_xq[i], w_xo[i]),
                   ln_x_g[i], ln_x_b[i])
        wg_b, wu_b, wd_b = w_gate[i].astype(BF16), w_up[i].astype(BF16), w_down[i].astype(BF16)
        yp = _swiglu(yp, wg_b, wu_b, wd_b, row1(ln_ff_g[i]), row1(ln_ff_b[i]), tm=tm)
        ys = _j_ln(ALPHA * ys + (jax.nn.silu(ys @ w_gate[i]) * (ys @ w_up[i])) @ w_down[i],
                   ln_ff_g[i], ln_ff_b[i])
    return (yp.reshape(bp, seq, d), ys,
            outs['fkp'], outs['fvp'], outs['flp'], outs['fks'], outs['fvs'], outs['fls'],
            outs['pps'], outs['pss'], outs['cps'], outs['css'], outs['hps'], outs['hss'], outs['gvs'],
            jnp.stack(mkp), jnp.stack(mvp))
```
